```python
import math
import jax, jax.numpy as jnp
from jax import lax
import numpy as np

D_MODEL = 2048
BATCH = 4
SEQ = 2048
DEPTH = 1

MEM_LEN = 256
HEAD_DIM = 64
CONV_DIM = 3 * D_MODEL // 8
CONV_WIDTH = 3
SWA_HEADS = 3 * D_MODEL // (8 * HEAD_DIM)
SWA_KV_HEADS = 4
SWA_GROUP = SWA_HEADS // SWA_KV_HEADS
SWA_DIM = SWA_HEADS * HEAD_DIM
KV_DIM = SWA_KV_HEADS * HEAD_DIM
WINDOW = 128
BLOCK = 128
MEM_HEADS = 4
MEM_DIM = D_MODEL // 4
MEM_HEAD_DIM = MEM_DIM // MEM_HEADS
MIX_DIM = CONV_DIM + SWA_DIM + MEM_DIM
PROJ_DIM = 3 * CONV_DIM + SWA_DIM + 2 * KV_DIM + MEM_DIM
D_FF = 256 * (-(-(8 * D_MODEL // 3) // 256))
N_BUCKETS = 32
MAX_DISTANCE = 128
ALPHA = (2.0 * DEPTH) ** 0.25
BETA = (8.0 * DEPTH) ** -0.25
LN_EPS = 1e-5
NEG_INF = -1e30

kernel_name = "hymba_conv_swa_mem_macaron_deepnorm"


def _layer_norm(x, g, b):
    xf = x.astype(jnp.float32)
    mu = xf.mean(-1, keepdims=True)
    var = jnp.square(xf - mu).mean(-1, keepdims=True)
    return ((xf - mu) * lax.rsqrt(var + LN_EPS) * g + b).astype(x.dtype)


def _swiglu(x, w_gate, w_up, w_down):
    return (jax.nn.silu(x @ w_gate) * (x @ w_up)) @ w_down


def _short_conv(u, w):
    s = u.shape[1]
    up = jnp.pad(u, ((0, 0), (CONV_WIDTH - 1, 0), (0, 0)))
    y = w[CONV_WIDTH - 1] * u
    for j in range(CONV_WIDTH - 1):
        y = y + w[j] * up[:, j:j + s]
    return y


def _t5_bucket(dist):
    max_exact = N_BUCKETS // 2
    d = jnp.maximum(dist, 1).astype(jnp.float32)
    large = max_exact + (jnp.log(d / max_exact) / math.log(MAX_DISTANCE / max_exact)
                         * (N_BUCKETS - max_exact)).astype(jnp.int32)
    large = jnp.minimum(large, N_BUCKETS - 1)
    return jnp.where(dist < max_exact, dist, large)


def _sliding_window_attention(q, k, v, rel_bias, sinks):
    b, s = q.shape[:2]
    nb = s // BLOCK
    qb = q.reshape(b, nb, BLOCK, SWA_KV_HEADS, SWA_GROUP, HEAD_DIM)

    def band(t):
        tb = t.reshape(b, nb, BLOCK, SWA_KV_HEADS, HEAD_DIM)
        prev = jnp.pad(tb, ((0, 0), (1, 0), (0, 0), (0, 0), (0, 0)))[:, :-1]
        return jnp.concatenate([prev, tb], axis=2)

    kb, vb = band(k), band(v)
    logits = jnp.einsum('bnqhgd,bnkhd->bnhgqk', qb, kb).astype(jnp.float32) * (HEAD_DIM ** -0.5)

    qi = jnp.arange(BLOCK)[:, None]
    kj = jnp.arange(2 * BLOCK)[None, :]
    dist = qi + BLOCK - kj
    in_window = (dist >= 0) & (dist < WINDOW)
    key_valid = (jnp.arange(nb)[:, None] * BLOCK - BLOCK + kj) >= 0
    mask = in_window[None] & key_valid[:, None, :]

    bias = rel_bias[_t5_bucket(jnp.maximum(dist, 0))]
    bias = bias.transpose(2, 0, 1).reshape(SWA_KV_HEADS, SWA_GROUP, BLOCK, 2 * BLOCK)
    logits = jnp.where(mask[None, :, None, None], logits + bias.astype(jnp.float32), NEG_INF)

    sink = sinks.astype(jnp.float32).reshape(1, 1, SWA_KV_HEADS, SWA_GROUP, 1, 1)
    m = jnp.maximum(logits.max(-1, keepdims=True), sink)
    p = jnp.exp(logits - m)
    probs = p / (p.sum(-1, keepdims=True) + jnp.exp(sink - m))
    out = jnp.einsum('bnhgqk,bnkhd->bnqhgd', probs.astype(vb.dtype), vb)
    return out.reshape(b, s, SWA_DIM)


def _memory_attention(qm, mem_k, mem_v):
    b, s = qm.shape[:2]
    q = qm.reshape(b, s, MEM_HEADS, MEM_HEAD_DIM)
    k = mem_k.reshape(b, MEM_LEN, MEM_HEADS, MEM_HEAD_DIM)
    v = mem_v.reshape(b, MEM_LEN, MEM_HEADS, MEM_HEAD_DIM)
    logits = jnp.einsum('bshd,bmhd->bhsm', q, k).astype(jnp.float32) * (MEM_HEAD_DIM ** -0.5)
    p = jax.nn.softmax(logits, axis=-1)
    return jnp.einsum('bhsm,bmhd->bshd', p.astype(v.dtype), v).reshape(b, s, MEM_DIM)


def setup_inputs(seed: int = 0) -> dict:
    key = jax.random.key(seed)
    ks = jax.random.split(key, 24)
    f32 = jnp.float32
    nrm = lambda k, shape, scale: jax.random.normal(k, shape, f32) * scale
    L = DEPTH
    col_scale = jnp.concatenate([
        jnp.ones((2 * CONV_DIM,), f32), jnp.full((CONV_DIM,), BETA, f32),
        jnp.ones((SWA_DIM + KV_DIM,), f32), jnp.full((KV_DIM,), BETA, f32),
        jnp.ones((MEM_DIM,), f32)])
    mem_scale = jnp.concatenate([jnp.ones((MEM_DIM,), f32), jnp.full((MEM_DIM,), BETA, f32)])
    return {
        "x": nrm(ks[0], (BATCH, SEQ, D_MODEL), 1.0),
        "mem": nrm(ks[1], (BATCH, MEM_LEN, D_MODEL), 1.0),
        "ln1_g": 1.0 + nrm(ks[2], (L, D_MODEL), 0.01),
        "ln1_b": nrm(ks[3], (L, D_MODEL), 0.01),
        "ffn1_w_gate": nrm(ks[4], (L, D_MODEL, D_FF), BETA * D_MODEL ** -0.5),
        "ffn1_w_up": nrm(ks[5], (L, D_MODEL, D_FF), BETA * D_MODEL ** -0.5),
        "ffn1_w_down": nrm(ks[6], (L, D_FF, D_MODEL), BETA * D_FF ** -0.5),
        "w_in": nrm(ks[7], (L, D_MODEL, PROJ_DIM), D_MODEL ** -0.5) * col_scale,
        "b_in": nrm(ks[8], (L, PROJ_DIM), 0.01),
        "conv_w": nrm(ks[9], (L, CONV_WIDTH, CONV_DIM), CONV_WIDTH ** -0.5),
        "sinks": nrm(ks[10], (L, SWA_HEADS), 0.5),
        "w_mem_kv": nrm(ks[11], (L, D_MODEL, 2 * MEM_DIM), D_MODEL ** -0.5) * mem_scale,
        "w_out": nrm(ks[12], (L, MIX_DIM, D_MODEL), BETA * MIX_DIM ** -0.5),
        "ln2_g": 1.0 + nrm(ks[13], (L, D_MODEL), 0.01),
        "ln2_b": nrm(ks[14], (L, D_MODEL), 0.01),
        "ffn2_w_gate": nrm(ks[15], (L, D_MODEL, D_FF), BETA * D_MODEL ** -0.5),
        "ffn2_w_up": nrm(ks[16], (L, D_MODEL, D_FF), BETA * D_MODEL ** -0.5),
        "ffn2_w_down": nrm(ks[17], (L, D_FF, D_MODEL), BETA * D_FF ** -0.5),
        "ln3_g": 1.0 + nrm(ks[18], (L, D_MODEL), 0.01),
        "ln3_b": nrm(ks[19], (L, D_MODEL), 0.01),
        "rel_bias": nrm(ks[20], (N_BUCKETS, SWA_HEADS), 0.2),
    }


def reference(x, mem, ln1_g, ln1_b, ffn1_w_gate, ffn1_w_up, ffn1_w_down, w_in, b_in,
              conv_w, sinks, w_mem_kv, w_out, ln2_g, ln2_b, ffn2_w_gate, ffn2_w_up,
              ffn2_w_down, ln3_g, ln3_b, rel_bias):
    splits = list(np.cumsum([CONV_DIM, CONV_DIM, CONV_DIM, SWA_DIM, KV_DIM, KV_DIM]))
    h = x
    for l in range(DEPTH):
        h = _layer_norm(ALPHA * h + 0.5 * _swiglu(h, ffn1_w_gate[l], ffn1_w_up[l], ffn1_w_down[l]),
                        ln1_g[l], ln1_b[l])
        proj = h @ w_in[l] + b_in[l]
        b_gate, c_gate, u, q, k, v, qm = jnp.split(proj, splits, axis=-1)
        conv_out = b_gate * _short_conv(c_gate * u, conv_w[l])
        swa_out = _sliding_window_attention(q, k, v, rel_bias, sinks[l])
        mem_k, mem_v = jnp.split(mem @ w_mem_kv[l], 2, axis=-1)
        mem_out = _memory_attention(qm, mem_k, mem_v)
        mix = jnp.concatenate([conv_out, swa_out, mem_out], axis=-1) @ w_out[l]
        h = _layer_norm(ALPHA * h + mix, ln2_g[l], ln2_b[l])
        h = _layer_norm(ALPHA * h + 0.5 * _swiglu(h, ffn2_w_gate[l], ffn2_w_up[l], ffn2_w_down[l]),
                        ln3_g[l], ln3_b[l])
    return h
```

```python
import functools
import math

import numpy as np
import jax
import jax.numpy as jnp
from jax import lax
from jax.experimental import pallas as pl
from jax.experimental.pallas import tpu as pltpu

D_MODEL = 2048
DEPTH = 1
MEM_LEN = 256
HEAD_DIM = 64
CONV_DIM = 3 * D_MODEL // 8
CONV_WIDTH = 3
SWA_HEADS = 3 * D_MODEL // (8 * HEAD_DIM)
SWA_KV_HEADS = 4
SWA_GROUP = SWA_HEADS // SWA_KV_HEADS
SWA_DIM = SWA_HEADS * HEAD_DIM
KV_DIM = SWA_KV_HEADS * HEAD_DIM
WINDOW = 128
BLOCK = 128
MEM_HEADS = 4
MEM_DIM = D_MODEL // 4
MEM_HEAD_DIM = MEM_DIM // MEM_HEADS
MIX_DIM = CONV_DIM + SWA_DIM + MEM_DIM
PROJ_DIM = 3 * CONV_DIM + SWA_DIM + 2 * KV_DIM + MEM_DIM
D_FF = 256 * (-(-(8 * D_MODEL // 3) // 256))
N_BUCKETS = 32
MAX_DISTANCE = 128
ALPHA = (2.0 * DEPTH) ** 0.25
LN_EPS = 1e-5
NEG_INF = -1e30

_OFF_BGATE = 0
_OFF_CGATE = CONV_DIM
_OFF_U = 2 * CONV_DIM
_OFF_Q = 3 * CONV_DIM
_OFF_K = _OFF_Q + SWA_DIM
_OFF_V = _OFF_K + KV_DIM
_OFF_QM = _OFF_V + KV_DIM

_V7X_VMEM_BYTES = 64 * 1024 * 1024
_VMEM_LIMIT = _V7X_VMEM_BYTES - 8 * 1024 * 1024
_SUBLANES = 8

_FFN_TM = 512
_FFN_TF = 512
_ROW_TM = 512

_BF16 = jnp.bfloat16
_F32 = jnp.float32


def _layer_norm(y, g, b):
    mu = jnp.mean(y, axis=-1, keepdims=True)
    d = y - mu
    var = jnp.mean(d * d, axis=-1, keepdims=True)
    return d * lax.rsqrt(var + LN_EPS) * g + b


def _dot(a, b):
    return jnp.dot(a, b, preferred_element_type=_F32)


def _dot_nt(a, b):
    return lax.dot_general(a, b, (((1,), (1,)), ((), ())), preferred_element_type=_F32)


def _ffn_ln_kernel(x_ref, wg_ref, wu_ref, wd_ref, g_ref, b_ref, o_ref, xb_ref, acc_ref):
    f = pl.program_id(1)

    @pl.when(f == 0)
    def _():
        xb_ref[...] = x_ref[...].astype(_BF16)
        acc_ref[...] = jnp.zeros_like(acc_ref)

    xb = xb_ref[...]
    gate = _dot(xb, wg_ref[...])
    up = _dot(xb, wu_ref[...])
    hid = (gate * jax.nn.sigmoid(gate) * up).astype(_BF16)
    acc_ref[...] += _dot(hid, wd_ref[...])

    @pl.when(f == pl.num_programs(1) - 1)
    def _():
        y = ALPHA * x_ref[...] + 0.5 * acc_ref[...]
        o_ref[...] = _layer_norm(y, g_ref[...], b_ref[...])


def _ffn_ln(x, wg, wu, wd, g, b):
    t, d = x.shape
    f = wg.shape[1]
    tm, tf = _FFN_TM, _FFN_TF
    return pl.pallas_call(
        _ffn_ln_kernel,
        out_shape=jax.ShapeDtypeStruct((t, d), _F32),
        grid=(t // tm, f // tf),
        in_specs=[
            pl.BlockSpec((tm, d), lambda i, j: (i, 0)),
            pl.BlockSpec((d, tf), lambda i, j: (0, j)),
            pl.BlockSpec((d, tf), lambda i, j: (0, j)),
            pl.BlockSpec((tf, d), lambda i, j: (j, 0)),
            pl.BlockSpec((1, d), lambda i, j: (0, 0)),
            pl.BlockSpec((1, d), lambda i, j: (0, 0)),
        ],
        out_specs=pl.BlockSpec((tm, d), lambda i, j: (i, 0)),
        scratch_shapes=[pltpu.VMEM((tm, d), _BF16), pltpu.VMEM((tm, d), _F32)],
        compiler_params=pltpu.CompilerParams(
            dimension_semantics=("parallel", "arbitrary"), vmem_limit_bytes=_VMEM_LIMIT),
        name="ffn_ln",
    )(x, wg, wu, wd, g, b)


def _proj_kernel(tiles_per_seq, h_ref, w_ref, b_ref, cw_ref,
                 conv_ref, q_ref, k_ref, v_ref, qm_ref, cu_ref):
    i = pl.program_id(0)
    tm = h_ref.shape[0]
    hb = h_ref[...].astype(_BF16)

    def seg(lo, width):
        return _dot(hb, w_ref[:, lo:lo + width]) + b_ref[:, lo:lo + width]

    @pl.when(i % tiles_per_seq == 0)
    def _():
        cu_ref[0:_SUBLANES, :] = jnp.zeros((_SUBLANES, CONV_DIM), _F32)

    cu = seg(_OFF_CGATE, CONV_DIM) * seg(_OFF_U, CONV_DIM)
    cu_ref[_SUBLANES:_SUBLANES + tm, :] = cu
    prev1 = cu_ref[_SUBLANES - 1:_SUBLANES - 1 + tm, :]
    prev2 = cu_ref[_SUBLANES - 2:_SUBLANES - 2 + tm, :]
    cw = cw_ref[...]
    y = cw[2:3, :] * cu + cw[0:1, :] * prev2 + cw[1:2, :] * prev1
    conv_ref[...] = (seg(_OFF_BGATE, CONV_DIM) * y).astype(_BF16)
    cu_ref[0:_SUBLANES, :] = cu_ref[tm:tm + _SUBLANES, :]

    q_ref[...] = (seg(_OFF_Q, SWA_DIM) * (HEAD_DIM ** -0.5)).astype(_BF16)
    k_ref[...] = seg(_OFF_K, KV_DIM).astype(_BF16)
    v_ref[...] = seg(_OFF_V, KV_DIM).astype(_BF16)
    qm_ref[...] = seg(_OFF_QM, MEM_DIM).astype(_BF16)


def _proj(h, w_in, b_in, conv_w, seq):
    t, d = h.shape
    tm = _ROW_TM
    assert seq % tm == 0
    row = lambda i: (i, 0)
    const = lambda i: (0, 0)
    widths = (CONV_DIM, SWA_DIM, KV_DIM, KV_DIM, MEM_DIM)
    return pl.pallas_call(
        functools.partial(_proj_kernel, seq // tm),
        out_shape=[jax.ShapeDtypeStruct((t, w), _BF16) for w in widths],
        grid=(t // tm,),
        in_specs=[
            pl.BlockSpec((tm, d), row),
            pl.BlockSpec((d, PROJ_DIM), const, pipeline_mode=pl.Buffered(1)),
            pl.BlockSpec((1, PROJ_DIM), const),
            pl.BlockSpec((CONV_WIDTH, CONV_DIM), const),
        ],
        out_specs=[pl.BlockSpec((tm, w), row) for w in widths],
        scratch_shapes=[pltpu.VMEM((tm + _SUBLANES, CONV_DIM), _F32)],
        compiler_params=pltpu.CompilerParams(
            dimension_semantics=("arbitrary",), vmem_limit_bytes=_VMEM_LIMIT),
        name="proj_conv",
    )(h, w_in, b_in, conv_w)


def _mem_kv_kernel(m_ref, w_ref, k_ref, v_ref):
    kv = _dot(m_ref[...].astype(_BF16), w_ref[...])
    k_ref[...] = kv[:, :MEM_DIM].astype(_BF16)
    v_ref[...] = kv[:, MEM_DIM:].astype(_BF16)


def _mem_kv(mem, w):
    t, d = mem.shape
    tm = MEM_LEN
    row = lambda i: (i, 0)
    return pl.pallas_call(
        _mem_kv_kernel,
        out_shape=[jax.ShapeDtypeStruct((t, MEM_DIM), _BF16)] * 2,
        grid=(t // tm,),
        in_specs=[pl.BlockSpec((tm, d), row),
                  pl.BlockSpec((d, 2 * MEM_DIM), lambda i: (0, 0))],
        out_specs=[pl.BlockSpec((tm, MEM_DIM), row)] * 2,
        compiler_params=pltpu.CompilerParams(
            dimension_semantics=("parallel",), vmem_limit_bytes=_VMEM_LIMIT),
        name="mem_kv",
    )(mem, w)


def _t5_bucket_table():
    qi = np.arange(BLOCK)[:, None]
    kj = np.arange(2 * BLOCK)[None, :]
    dist = qi + BLOCK - kj
    in_window = (dist >= 0) & (dist < WINDOW)
    dc = np.maximum(dist, 0)
    max_exact = N_BUCKETS // 2
    dl = np.maximum(dc, 1).astype(np.float32)
    large = max_exact + (np.log(dl / max_exact) / math.log(MAX_DISTANCE / max_exact)
                         * (N_BUCKETS - max_exact)).astype(np.int32)
    large = np.minimum(large, N_BUCKETS - 1)
    bucket = np.where(dc < max_exact, dc, large)
    return np.where(in_window, bucket, -1).astype(np.int32)


def _swa_kernel(q_ref, kp_ref, kc_ref, vp_ref, vc_ref, bucket_ref, relb_ref, sink_ref,
                o_ref, bias_ref):
    b = pl.program_id(0)
    n = pl.program_id(1)

    @pl.when((b == 0) & (n == 0))
    def _():
        bucket = bucket_ref[...]
        for h in range(SWA_HEADS):
            t = jnp.full(bucket.shape, NEG_INF, _F32)
            for j in range(N_BUCKETS):
                t = jnp.where(bucket == j, relb_ref[j, h], t)
            bias_ref[h] = t

    prev_pen = jnp.where(n == 0, NEG_INF, 0.0).astype(_F32)

    outs = []
    for h in range(SWA_HEADS):
        g = h // SWA_GROUP
        qh = q_ref[:, h * HEAD_DIM:(h + 1) * HEAD_DIM]
        ksl = slice(g * HEAD_DIM, (g + 1) * HEAD_DIM)
        sink = sink_ref[h]
        s_p = _dot_nt(qh, kp_ref[:, ksl]) + (bias_ref[h, :, 0:BLOCK] + prev_pen)
        s_c = _dot_nt(qh, kc_ref[:, ksl]) + bias_ref[h, :, BLOCK:2 * BLOCK]
        m = jnp.maximum(jnp.maximum(jnp.max(s_p, axis=-1, keepdims=True),
                                    jnp.max(s_c, axis=-1, keepdims=True)), sink)
        p_p = jnp.exp(s_p - m)
        p_c = jnp.exp(s_c - m)
        denom = (jnp.sum(p_p, axis=-1, keepdims=True) + jnp.sum(p_c, axis=-1, keepdims=True)
                 + jnp.exp(sink - m))
        o = _dot(p_p.astype(_BF16), vp_ref[:, ksl]) + _dot(p_c.astype(_BF16), vc_ref[:, ksl])
        outs.append(o / denom)
    o_ref[...] = jnp.concatenate(outs, axis=-1).astype(_BF16)


def _swa(q, k, v, rel_bias, sinks, batch, seq):
    nb = seq // BLOCK
    cur = lambda b, n: (b * nb + n, 0)
    prev = lambda b, n: (b * nb + jnp.maximum(n - 1, 0), 0)
    bucket = jnp.asarray(_t5_bucket_table())
    smem = pl.BlockSpec(memory_space=pltpu.SMEM)
    return pl.pallas_call(
        _swa_kernel,
        out_shape=jax.ShapeDtypeStruct((batch * seq, SWA_DIM), _BF16),
        grid=(batch, nb),
        in_specs=[
            pl.BlockSpec((BLOCK, SWA_DIM), cur),
            pl.BlockSpec((BLOCK, KV_DIM), prev),
            pl.BlockSpec((BLOCK, KV_DIM), cur),
            pl.BlockSpec((BLOCK, KV_DIM), prev),
            pl.BlockSpec((BLOCK, KV_DIM), cur),
            pl.BlockSpec((BLOCK, 2 * BLOCK), lambda b, n: (0, 0)),
            smem,
            smem,
        ],
        out_specs=pl.BlockSpec((BLOCK, SWA_DIM), cur),
        scratch_shapes=[pltpu.VMEM((SWA_HEADS, BLOCK, 2 * BLOCK), _F32)],
        compiler_params=pltpu.CompilerParams(
            dimension_semantics=("arbitrary", "arbitrary"), vmem_limit_bytes=_VMEM_LIMIT),
        name="swa",
    )(q, k, k, v, v, bucket, rel_bias, sinks)


def _mem_attn_kernel(q_ref, k_ref, v_ref, o_ref):
    outs = []
    for h in range(MEM_HEADS):
        sl = slice(h * MEM_HEAD_DIM, (h + 1) * MEM_HEAD_DIM)
        s = _dot_nt(q_ref[:, sl], k_ref[:, sl]) * (MEM_HEAD_DIM ** -0.5)
        p = jnp.exp(s - jnp.max(s, axis=-1, keepdims=True))
        denom = jnp.sum(p, axis=-1, keepdims=True)
        outs.append(_dot(p.astype(_BF16), v_ref[:, sl]) / denom)
    o_ref[...] = jnp.concatenate(outs, axis=-1).astype(_BF16)


def _mem_attn(qm, mem_k, mem_v, batch, seq):
    tq = _ROW_TM
    per_seq = seq // tq
    return pl.pallas_call(
        _mem_attn_kernel,
        out_shape=jax.ShapeDtypeStruct((batch * seq, MEM_DIM), _BF16),
        grid=(batch, per_seq),
        in_specs=[
            pl.BlockSpec((tq, MEM_DIM), lambda b, j: (b * per_seq + j, 0)),
            pl.BlockSpec((MEM_LEN, MEM_DIM), lambda b, j: (b, 0)),
            pl.BlockSpec((MEM_LEN, MEM_DIM), lambda b, j: (b, 0)),
        ],
        out_specs=pl.BlockSpec((tq, MEM_DIM), lambda b, j: (b * per_seq + j, 0)),
        compiler_params=pltpu.CompilerParams(
            dimension_semantics=("parallel", "parallel"), vmem_limit_bytes=_VMEM_LIMIT),
        name="mem_attn",
    )(qm, mem_k, mem_v)


def _out_ln_kernel(conv_ref, swa_ref, mem_ref, h_ref, w_ref, g_ref, b_ref, o_ref):
    mix = jnp.concatenate([conv_ref[...], swa_ref[...], mem_ref[...]], axis=-1)
    y = ALPHA * h_ref[...] + _dot(mix, w_ref[...])
    o_ref[...] = _layer_norm(y, g_ref[...], b_ref[...])


def _out_ln(conv, swa, memo, h, w_out, g, b):
    t, d = h.shape
    tm = _ROW_TM
    row = lambda i: (i, 0)
    const = lambda i: (0, 0)
    return pl.pallas_call(
        _out_ln_kernel,
        out_shape=jax.ShapeDtypeStruct((t, d), _F32),
        grid=(t // tm,),
        in_specs=[
            pl.BlockSpec((tm, CONV_DIM), row),
            pl.BlockSpec((tm, SWA_DIM), row),
            pl.BlockSpec((tm, MEM_DIM), row),
            pl.BlockSpec((tm, d), row),
            pl.BlockSpec((MIX_DIM, d), const),
            pl.BlockSpec((1, d), const),
            pl.BlockSpec((1, d), const),
        ],
        out_specs=pl.BlockSpec((tm, d), row),
        compiler_params=pltpu.CompilerParams(
            dimension_semantics=("parallel",), vmem_limit_bytes=_VMEM_LIMIT),
        name="out_ln",
    )(conv, swa, memo, h, w_out, g, b)


def kernel(x, mem, ln1_g, ln1_b, ffn1_w_gate, ffn1_w_up, ffn1_w_down, w_in, b_in, conv_w, sinks, w_mem_kv, w_out, ln2_g, ln2_b, ffn2_w_gate, ffn2_w_up, ffn2_w_down, ln3_g, ln3_b, rel_bias):
    batch, seq, d = x.shape
    h = x.reshape(batch * seq, d)
    mem2 = mem.reshape(batch * MEM_LEN, d)
    bf = lambda w: w.astype(_BF16)
    for l in range(DEPTH):
        h = _ffn_ln(h, bf(ffn1_w_gate[l]), bf(ffn1_w_up[l]), bf(ffn1_w_down[l]),
                    ln1_g[l][None], ln1_b[l][None])
        conv, q, k, v, qm = _proj(h, bf(w_in[l]), b_in[l][None], conv_w[l], seq)
        mem_k, mem_v = _mem_kv(mem2, bf(w_mem_kv[l]))
        swa = _swa(q, k, v, rel_bias, sinks[l], batch, seq)
        memo = _mem_attn(qm, mem_k, mem_v, batch, seq)
        h = _out_ln(conv, swa, memo, h, bf(w_out[l]), ln2_g[l][None], ln2_b[l][None])
        h = _ffn_ln(h, bf(ffn2_w_gate[l]), bf(ffn2_w_up[l]), bf(ffn2_w_down[l]),
                    ln3_g[l][None], ln3_b[l][None])
    return h.reshape(batch, seq, d)
```

```python
import functools
import math

import numpy as np
import jax
import jax.numpy as jnp
from jax import lax
from jax.experimental import pallas as pl
from jax.experimental.pallas import tpu as pltpu

D_MODEL = 2048
DEPTH = 1
MEM_LEN = 256
HEAD_DIM = 64
CONV_DIM = 3 * D_MODEL // 8
CONV_WIDTH = 3
SWA_HEADS = 3 * D_MODEL // (8 * HEAD_DIM)
SWA_KV_HEADS = 4
SWA_GROUP = SWA_HEADS // SWA_KV_HEADS
SWA_DIM = SWA_HEADS * HEAD_DIM
KV_DIM = SWA_KV_HEADS * HEAD_DIM
WINDOW = 128
BLOCK = 128
MEM_HEADS = 4
MEM_DIM = D_MODEL // 4
MEM_HEAD_DIM = MEM_DIM // MEM_HEADS
MIX_DIM = CONV_DIM + SWA_DIM + MEM_DIM
PROJ_DIM = 3 * CONV_DIM + SWA_DIM + 2 * KV_DIM + MEM_DIM
D_FF = 256 * (-(-(8 * D_MODEL // 3) // 256))
N_BUCKETS = 32
MAX_DISTANCE = 128
ALPHA = (2.0 * DEPTH) ** 0.25
LN_EPS = 1e-5
NEG_INF = -1e30

_OFF_BGATE = 0
_OFF_CGATE = CONV_DIM
_OFF_U = 2 * CONV_DIM
_OFF_Q = 3 * CONV_DIM
_OFF_K = _OFF_Q + SWA_DIM
_OFF_V = _OFF_K + KV_DIM
_OFF_QM = _OFF_V + KV_DIM

_V7X_VMEM_BYTES = 64 * 1024 * 1024
_VMEM_LIMIT = _V7X_VMEM_BYTES - 6 * 1024 * 1024
_SUBLANES = 8

_FFN_TM = 1024
_FFN_TF = 512
_ROW_TM = 512
_OUT_TM = 1024
_LN_ROWS = 256

_BF16 = jnp.bfloat16
_F32 = jnp.float32


def _layer_norm(y, g, b):
    mu = jnp.mean(y, axis=-1, keepdims=True)
    d = y - mu
    var = jnp.mean(d * d, axis=-1, keepdims=True)
    return d * lax.rsqrt(var + LN_EPS) * g + b


def _dot(a, b):
    return jnp.dot(a, b, preferred_element_type=_F32)


def _dot_nt(a, b):
    return lax.dot_general(a, b, (((1,), (1,)), ((), ())), preferred_element_type=_F32)


def _ffn_ln_kernel(x_hbm, wg_ref, wu_ref, wd_ref, g_ref, b_ref, o_ref, xb_ref, xf_ref, sem):
    i = pl.program_id(0)
    f = pl.program_id(1)
    tm = xf_ref.shape[0]

    def x_copy(tile):
        return pltpu.make_async_copy(x_hbm.at[pl.ds(tile * tm, tm), :], xf_ref, sem)

    @pl.when((i == 0) & (f == 0))
    def _():
        x_copy(0).start()

    @pl.when(f == 0)
    def _():
        x_copy(i).wait()
        x = xf_ref[...]
        xb_ref[...] = x.astype(_BF16)
        o_ref[...] = (2.0 * ALPHA) * x

    @pl.when((f == 1) & (i + 1 < pl.num_programs(0)))
    def _():
        x_copy(i + 1).start()

    xb = xb_ref[...]
    gate = _dot(xb, wg_ref[...])
    up = _dot(xb, wu_ref[...])
    hid = (gate * jax.nn.sigmoid(gate) * up).astype(_BF16)
    o_ref[...] += _dot(hid, wd_ref[...])

    @pl.when(f == pl.num_programs(1) - 1)
    def _():
        o_ref[...] = _layer_norm(0.5 * o_ref[...], g_ref[...], b_ref[...])


def _ffn_ln(x, wg, wu, wd, g, b):
    t, d = x.shape
    f = wg.shape[1]
    tm, tf = _FFN_TM, _FFN_TF
    return pl.pallas_call(
        _ffn_ln_kernel,
        out_shape=jax.ShapeDtypeStruct((t, d), _F32),
        grid=(t // tm, f // tf),
        in_specs=[
            pl.BlockSpec(memory_space=pl.ANY),
            pl.BlockSpec((d, tf), lambda i, j: (0, j)),
            pl.BlockSpec((d, tf), lambda i, j: (0, j)),
            pl.BlockSpec((tf, d), lambda i, j: (j, 0)),
            pl.BlockSpec((1, d), lambda i, j: (0, 0)),
            pl.BlockSpec((1, d), lambda i, j: (0, 0)),
        ],
        out_specs=pl.BlockSpec((tm, d), lambda i, j: (i, 0)),
        scratch_shapes=[pltpu.VMEM((tm, d), _BF16), pltpu.VMEM((tm, d), _F32),
                        pltpu.SemaphoreType.DMA(())],
        compiler_params=pltpu.CompilerParams(
            dimension_semantics=("arbitrary", "arbitrary"), vmem_limit_bytes=_VMEM_LIMIT),
        name="ffn_ln",
    )(x, wg, wu, wd, g, b)


def _proj_kernel(tiles_per_seq, h_ref, w_ref, b_ref, cw_ref,
                 conv_ref, q_ref, k_ref, v_ref, qm_ref, cu_ref):
    i = pl.program_id(0)
    tm = h_ref.shape[0]
    hb = h_ref[...].astype(_BF16)

    def seg(lo, width):
        return _dot(hb, w_ref[:, lo:lo + width]) + b_ref[:, lo:lo + width]

    @pl.when(i % tiles_per_seq == 0)
    def _():
        cu_ref[0:_SUBLANES, :] = jnp.zeros((_SUBLANES, CONV_DIM), _F32)

    cu = seg(_OFF_CGATE, CONV_DIM) * seg(_OFF_U, CONV_DIM)
    cu_ref[_SUBLANES:_SUBLANES + tm, :] = cu
    prev1 = cu_ref[_SUBLANES - 1:_SUBLANES - 1 + tm, :]
    prev2 = cu_ref[_SUBLANES - 2:_SUBLANES - 2 + tm, :]
    cw = cw_ref[...]
    y = cw[2:3, :] * cu + cw[0:1, :] * prev2 + cw[1:2, :] * prev1
    conv_ref[...] = (seg(_OFF_BGATE, CONV_DIM) * y).astype(_BF16)
    cu_ref[0:_SUBLANES, :] = cu_ref[tm:tm + _SUBLANES, :]

    q_ref[...] = (seg(_OFF_Q, SWA_DIM) * (HEAD_DIM ** -0.5)).astype(_BF16)
    k_ref[...] = seg(_OFF_K, KV_DIM).astype(_BF16)
    v_ref[...] = seg(_OFF_V, KV_DIM).astype(_BF16)
    qm_ref[...] = seg(_OFF_QM, MEM_DIM).astype(_BF16)


def _proj(h, w_in, b_in, conv_w, seq):
    t, d = h.shape
    tm = _ROW_TM
    assert seq % tm == 0
    row = lambda i: (i, 0)
    const = lambda i: (0, 0)
    widths = (CONV_DIM, SWA_DIM, KV_DIM, KV_DIM, MEM_DIM)
    return pl.pallas_call(
        functools.partial(_proj_kernel, seq // tm),
        out_shape=[jax.ShapeDtypeStruct((t, w), _BF16) for w in widths],
        grid=(t // tm,),
        in_specs=[
            pl.BlockSpec((tm, d), row),
            pl.BlockSpec((d, PROJ_DIM), const, pipeline_mode=pl.Buffered(1)),
            pl.BlockSpec((1, PROJ_DIM), const),
            pl.BlockSpec((CONV_WIDTH, CONV_DIM), const),
        ],
        out_specs=[pl.BlockSpec((tm, w), row) for w in widths],
        scratch_shapes=[pltpu.VMEM((tm + _SUBLANES, CONV_DIM), _F32)],
        compiler_params=pltpu.CompilerParams(
            dimension_semantics=("arbitrary",), vmem_limit_bytes=_VMEM_LIMIT),
        name="proj_conv",
    )(h, w_in, b_in, conv_w)


def _mem_kv_kernel(m_ref, w_ref, k_ref, v_ref):
    kv = _dot(m_ref[...].astype(_BF16), w_ref[...])
    k_ref[...] = kv[:, :MEM_DIM].astype(_BF16)
    v_ref[...] = kv[:, MEM_DIM:].astype(_BF16)


def _mem_kv(mem, w):
    t, d = mem.shape
    tm = MEM_LEN
    row = lambda i: (i, 0)
    return pl.pallas_call(
        _mem_kv_kernel,
        out_shape=[jax.ShapeDtypeStruct((t, MEM_DIM), _BF16)] * 2,
        grid=(t // tm,),
        in_specs=[pl.BlockSpec((tm, d), row),
                  pl.BlockSpec((d, 2 * MEM_DIM), lambda i: (0, 0))],
        out_specs=[pl.BlockSpec((tm, MEM_DIM), row)] * 2,
        compiler_params=pltpu.CompilerParams(
            dimension_semantics=("parallel",), vmem_limit_bytes=_VMEM_LIMIT),
        name="mem_kv",
    )(mem, w)


def _t5_bucket_table():
    qi = np.arange(BLOCK)[:, None]
    kj = np.arange(2 * BLOCK)[None, :]
    dist = qi + BLOCK - kj
    in_window = (dist >= 0) & (dist < WINDOW)
    dc = np.maximum(dist, 0)
    max_exact = N_BUCKETS // 2
    dl = np.maximum(dc, 1).astype(np.float32)
    large = max_exact + (np.log(dl / max_exact) / math.log(MAX_DISTANCE / max_exact)
                         * (N_BUCKETS - max_exact)).astype(np.int32)
    large = np.minimum(large, N_BUCKETS - 1)
    bucket = np.where(dc < max_exact, dc, large)
    return np.where(in_window, bucket, -1).astype(np.int32)


def _swa_kernel(q_ref, kp_ref, kc_ref, vp_ref, vc_ref, bucket_ref, relb_ref, sink_ref,
                o_ref, bias_ref):
    b = pl.program_id(0)
    n = pl.program_id(1)

    @pl.when((b == 0) & (n == 0))
    def _():
        bucket = bucket_ref[...]
        is_prev = lax.broadcasted_iota(jnp.int32, bucket.shape, 1) < BLOCK
        for h in range(SWA_HEADS):
            t = jnp.full(bucket.shape, NEG_INF, _F32)
            for j in range(N_BUCKETS):
                t = jnp.where(bucket == j, relb_ref[j, h], t)
            bias_ref[0, h] = t
            bias_ref[1, h] = jnp.where(is_prev, NEG_INF, t)

    first = (n == 0).astype(jnp.int32)
    kband = jnp.concatenate([kp_ref[...], kc_ref[...]], axis=0)
    vband = jnp.concatenate([vp_ref[...], vc_ref[...]], axis=0)

    logits = []
    for g in range(SWA_KV_HEADS):
        qg = jnp.concatenate(
            [q_ref[:, h * HEAD_DIM:(h + 1) * HEAD_DIM]
             for h in range(g * SWA_GROUP, (g + 1) * SWA_GROUP)], axis=0)
        logits.append(_dot_nt(qg, kband[:, g * HEAD_DIM:(g + 1) * HEAD_DIM]))

    probs, denoms = [], []
    for g in range(SWA_KV_HEADS):
        pg = []
        for j in range(SWA_GROUP):
            h = g * SWA_GROUP + j
            sink = sink_ref[h]
            s = logits[g][j * BLOCK:(j + 1) * BLOCK] + bias_ref[first, h]
            m = jnp.maximum(jnp.max(s, axis=-1, keepdims=True), sink)
            p = jnp.exp(s - m)
            denoms.append(jnp.sum(p, axis=-1, keepdims=True) + jnp.exp(sink - m))
            pg.append(p.astype(_BF16))
        probs.append(jnp.concatenate(pg, axis=0))

    outs = []
    for g in range(SWA_KV_HEADS):
        og = _dot(probs[g], vband[:, g * HEAD_DIM:(g + 1) * HEAD_DIM])
        for j in range(SWA_GROUP):
            outs.append(og[j * BLOCK:(j + 1) * BLOCK] / denoms[g * SWA_GROUP + j])
    o_ref[...] = jnp.concatenate(outs, axis=-1).astype(_BF16)


def _swa(q, k, v, rel_bias, sinks, batch, seq):
    nb = seq // BLOCK
    cur = lambda b, n: (b * nb + n, 0)
    prev = lambda b, n: (b * nb + jnp.maximum(n - 1, 0), 0)
    bucket = jnp.asarray(_t5_bucket_table())
    smem = pl.BlockSpec(memory_space=pltpu.SMEM)
    return pl.pallas_call(
        _swa_kernel,
        out_shape=jax.ShapeDtypeStruct((batch * seq, SWA_DIM), _BF16),
        grid=(batch, nb),
        in_specs=[
            pl.BlockSpec((BLOCK, SWA_DIM), cur),
            pl.BlockSpec((BLOCK, KV_DIM), prev),
            pl.BlockSpec((BLOCK, KV_DIM), cur),
            pl.BlockSpec((BLOCK, KV_DIM), prev),
            pl.BlockSpec((BLOCK, KV_DIM), cur),
            pl.BlockSpec((BLOCK, 2 * BLOCK), lambda b, n: (0, 0)),
            smem,
            smem,
        ],
        out_specs=pl.BlockSpec((BLOCK, SWA_DIM), cur),
        scratch_shapes=[pltpu.VMEM((2, SWA_HEADS, BLOCK, 2 * BLOCK), _F32)],
        compiler_params=pltpu.CompilerParams(
            dimension_semantics=("arbitrary", "arbitrary"), vmem_limit_bytes=_VMEM_LIMIT),
        name="swa",
    )(q, k, k, v, v, bucket, rel_bias, sinks)


def _mem_attn_kernel(q_ref, k_ref, v_ref, o_ref):
    outs = []
    for h in range(MEM_HEADS):
        sl = slice(h * MEM_HEAD_DIM, (h + 1) * MEM_HEAD_DIM)
        s = _dot_nt(q_ref[:, sl], k_ref[:, sl]) * (MEM_HEAD_DIM ** -0.5)
        p = jnp.exp(s - jnp.max(s, axis=-1, keepdims=True))
        denom = jnp.sum(p, axis=-1, keepdims=True)
        outs.append(_dot(p.astype(_BF16), v_ref[:, sl]) / denom)
    o_ref[...] = jnp.concatenate(outs, axis=-1).astype(_BF16)


def _mem_attn(qm, mem_k, mem_v, batch, seq):
    tq = _ROW_TM
    per_seq = seq // tq
    return pl.pallas_call(
        _mem_attn_kernel,
        out_shape=jax.ShapeDtypeStruct((batch * seq, MEM_DIM), _BF16),
        grid=(batch, per_seq),
        in_specs=[
            pl.BlockSpec((tq, MEM_DIM), lambda b, j: (b * per_seq + j, 0)),
            pl.BlockSpec((MEM_LEN, MEM_DIM), lambda b, j: (b, 0)),
            pl.BlockSpec((MEM_LEN, MEM_DIM), lambda b, j: (b, 0)),
        ],
        out_specs=pl.BlockSpec((tq, MEM_DIM), lambda b, j: (b * per_seq + j, 0)),
        compiler_params=pltpu.CompilerParams(
            dimension_semantics=("parallel", "parallel"), vmem_limit_bytes=_VMEM_LIMIT),
        name="mem_attn",
    )(qm, mem_k, mem_v)


def _out_ln_kernel(conv_ref, swa_ref, mem_ref, h_ref, w_ref, g_ref, b_ref, o_ref):
    for r in range(h_ref.shape[0] // _LN_ROWS):
        rows = slice(r * _LN_ROWS, (r + 1) * _LN_ROWS)
        mix = jnp.concatenate([conv_ref[rows, :], swa_ref[rows, :], mem_ref[rows, :]], axis=-1)
        y = ALPHA * h_ref[rows, :] + _dot(mix, w_ref[...])
        o_ref[rows, :] = _layer_norm(y, g_ref[...], b_ref[...])


def _out_ln(conv, swa, memo, h, w_out, g, b):
    t, d = h.shape
    tm = _OUT_TM
    row = lambda i: (i, 0)
    const = lambda i: (0, 0)
    return pl.pallas_call(
        _out_ln_kernel,
        out_shape=jax.ShapeDtypeStruct((t, d), _F32),
        grid=(t // tm,),
        in_specs=[
            pl.BlockSpec((tm, CONV_DIM), row),
            pl.BlockSpec((tm, SWA_DIM), row),
            pl.BlockSpec((tm, MEM_DIM), row),
            pl.BlockSpec((tm, d), row),
            pl.BlockSpec((MIX_DIM, d), const, pipeline_mode=pl.Buffered(1)),
            pl.BlockSpec((1, d), const),
            pl.BlockSpec((1, d), const),
        ],
        out_specs=pl.BlockSpec((tm, d), row),
        compiler_params=pltpu.CompilerParams(
            dimension_semantics=("parallel",), vmem_limit_bytes=_VMEM_LIMIT),
        name="out_ln",
    )(conv, swa, memo, h, w_out, g, b)


def kernel(x, mem, ln1_g, ln1_b, ffn1_w_gate, ffn1_w_up, ffn1_w_down, w_in, b_in, conv_w, sinks, w_mem_kv, w_out, ln2_g, ln2_b, ffn2_w_gate, ffn2_w_up, ffn2_w_down, ln3_g, ln3_b, rel_bias):
    batch, seq, d = x.shape
    h = x.reshape(batch * seq, d)
    mem2 = mem.reshape(batch * MEM_LEN, d)
    bf = lambda w: w.astype(_BF16)
    for l in range(DEPTH):
        h = _ffn_ln(h, bf(ffn1_w_gate[l]), bf(ffn1_w_up[l]), bf(ffn1_w_down[l]),
                    ln1_g[l][None], ln1_b[l][None])
        conv, q, k, v, qm = _proj(h, bf(w_in[l]), b_in[l][None], conv_w[l], seq)
        mem_k, mem_v = _mem_kv(mem2, bf(w_mem_kv[l]))
        swa = _swa(q, k, v, rel_bias, sinks[l], batch, seq)
        memo = _mem_attn(qm, mem_k, mem_v, batch, seq)
        h = _out_ln(conv, swa, memo, h, bf(w_out[l]), ln2_g[l][None], ln2_b[l][None])
        h = _ffn_ln(h, bf(ffn2_w_gate[l]), bf(ffn2_w_up[l]), bf(ffn2_w_down[l]),
                    ln3_g[l][None], ln3_b[l][None])
    return h.reshape(batch, seq, d)
```

```python
import functools
import math

import numpy as np
import jax
import jax.numpy as jnp
from jax import lax
from jax.experimental import pallas as pl
from jax.experimental.pallas import tpu as pltpu

D_MODEL = 2048
DEPTH = 1
MEM_LEN = 256
HEAD_DIM = 64
CONV_DIM = 3 * D_MODEL // 8
CONV_WIDTH = 3
SWA_HEADS = 3 * D_MODEL // (8 * HEAD_DIM)
SWA_KV_HEADS = 4
SWA_GROUP = SWA_HEADS // SWA_KV_HEADS
SWA_DIM = SWA_HEADS * HEAD_DIM
KV_DIM = SWA_KV_HEADS * HEAD_DIM
WINDOW = 128
BLOCK = 128
MEM_HEADS = 4
MEM_DIM = D_MODEL // 4
MEM_HEAD_DIM = MEM_DIM // MEM_HEADS
MIX_DIM = CONV_DIM + SWA_DIM + MEM_DIM
PROJ_DIM = 3 * CONV_DIM + SWA_DIM + 2 * KV_DIM + MEM_DIM
D_FF = 256 * (-(-(8 * D_MODEL // 3) // 256))
N_BUCKETS = 32
MAX_DISTANCE = 128
ALPHA = (2.0 * DEPTH) ** 0.25
LN_EPS = 1e-5
NEG_INF = -1e30

_OFF_BGATE = 0
_OFF_CGATE = CONV_DIM
_OFF_U = 2 * CONV_DIM
_OFF_Q = 3 * CONV_DIM
_OFF_K = _OFF_Q + SWA_DIM
_OFF_V = _OFF_K + KV_DIM
_OFF_QM = _OFF_V + KV_DIM

_V7X_VMEM_BYTES = 64 * 1024 * 1024
_VMEM_LIMIT = _V7X_VMEM_BYTES - 4 * 1024 * 1024
_SUBLANES = 8
_BF16_SUBLANES = 16

_FFN_TM = 1024
_FFN_TF = 512
_ROW_TM = 512
_OUT_TM = 1024
_LN_ROWS = 256

_BF16 = jnp.bfloat16
_F32 = jnp.float32


def _layer_norm(y, g, b):
    mu = jnp.mean(y, axis=-1, keepdims=True)
    d = y - mu
    var = jnp.mean(d * d, axis=-1, keepdims=True)
    return d * lax.rsqrt(var + LN_EPS) * g + b


def _dot(a, b):
    return jnp.dot(a, b, preferred_element_type=_F32)


def _dot_nt(a, b):
    return lax.dot_general(a, b, (((1,), (1,)), ((), ())), preferred_element_type=_F32)


def _cast_specs(casts, n_steps, step_of):
    specs = []
    for w in casts:
        rb = _BF16_SUBLANES
        while w.shape[0] % rb or w.shape[0] // rb > n_steps:
            rb += _BF16_SUBLANES
        last = w.shape[0] // rb - 1
        specs.append(pl.BlockSpec(
            (rb, w.shape[1]), lambda *ids, last=last: (jnp.minimum(step_of(*ids), last), 0)))
    return specs


def _cast_rows(cast_in, cast_out):
    for src, dst in zip(cast_in, cast_out):
        dst[...] = src[...].astype(_BF16)


def _ffn_ln_kernel(n_cast, x_hbm, wg_ref, wu_ref, wd_ref, g_ref, b_ref, *refs):
    cast_in = refs[:n_cast]
    o_ref = refs[n_cast]
    cast_out = refs[n_cast + 1:2 * n_cast + 1]
    xb_ref, xf_ref, sem = refs[2 * n_cast + 1:]
    i = pl.program_id(0)
    f = pl.program_id(1)
    tm = xf_ref.shape[0]

    _cast_rows(cast_in, cast_out)

    def x_copy(tile):
        return pltpu.make_async_copy(x_hbm.at[pl.ds(tile * tm, tm), :], xf_ref, sem)

    @pl.when((i == 0) & (f == 0))
    def _():
        x_copy(0).start()

    @pl.when(f == 0)
    def _():
        x_copy(i).wait()
        x = xf_ref[...]
        xb_ref[...] = x.astype(_BF16)
        o_ref[...] = (2.0 * ALPHA) * x

    @pl.when((f == 1) & (i + 1 < pl.num_programs(0)))
    def _():
        x_copy(i + 1).start()

    xb = xb_ref[...]
    gate = _dot(xb, wg_ref[...])
    up = _dot(xb, wu_ref[...])
    hid = (gate * jax.nn.sigmoid(gate) * up).astype(_BF16)
    o_ref[...] += _dot(hid, wd_ref[...])

    @pl.when(f == pl.num_programs(1) - 1)
    def _():
        o_ref[...] = _layer_norm(0.5 * o_ref[...], g_ref[...], b_ref[...])


def _ffn_ln(x, wg, wu, wd, g, b, casts=()):
    t, d = x.shape
    f = wg.shape[1]
    tm, tf = _FFN_TM, _FFN_TF
    nf = f // tf
    cast_specs = _cast_specs(casts, (t // tm) * nf, lambda i, j: i * nf + j)
    outs = pl.pallas_call(
        functools.partial(_ffn_ln_kernel, len(casts)),
        out_shape=[jax.ShapeDtypeStruct((t, d), _F32)]
        + [jax.ShapeDtypeStruct(w.shape, _BF16) for w in casts],
        grid=(t // tm, nf),
        in_specs=[
            pl.BlockSpec(memory_space=pl.ANY),
            pl.BlockSpec((d, tf), lambda i, j: (0, j)),
            pl.BlockSpec((d, tf), lambda i, j: (0, j)),
            pl.BlockSpec((tf, d), lambda i, j: (j, 0)),
            pl.BlockSpec((1, d), lambda i, j: (0, 0)),
            pl.BlockSpec((1, d), lambda i, j: (0, 0)),
        ] + cast_specs,
        out_specs=[pl.BlockSpec((tm, d), lambda i, j: (i, 0))] + cast_specs,
        scratch_shapes=[pltpu.VMEM((tm, d), _BF16), pltpu.VMEM((tm, d), _F32),
                        pltpu.SemaphoreType.DMA(())],
        compiler_params=pltpu.CompilerParams(
            dimension_semantics=("arbitrary", "arbitrary"), vmem_limit_bytes=_VMEM_LIMIT),
        name="ffn_ln",
    )(x, wg, wu, wd, g, b, *casts)
    return outs


def _proj_kernel(tiles_per_seq, n_cast, h_ref, w_ref, b_ref, cw_ref, *refs):
    cast_in = refs[:n_cast]
    conv_ref, q_ref, k_ref, v_ref, qm_ref = refs[n_cast:n_cast + 5]
    cast_out = refs[n_cast + 5:2 * n_cast + 5]
    cu_ref, = refs[2 * n_cast + 5:]
    _cast_rows(cast_in, cast_out)
    i = pl.program_id(0)
    tm = h_ref.shape[0]
    hb = h_ref[...].astype(_BF16)

    def seg(lo, width):
        return _dot(hb, w_ref[:, lo:lo + width]) + b_ref[:, lo:lo + width]

    @pl.when(i % tiles_per_seq == 0)
    def _():
        cu_ref[0:_SUBLANES, :] = jnp.zeros((_SUBLANES, CONV_DIM), _F32)

    cu = seg(_OFF_CGATE, CONV_DIM) * seg(_OFF_U, CONV_DIM)
    cu_ref[_SUBLANES:_SUBLANES + tm, :] = cu
    prev1 = cu_ref[_SUBLANES - 1:_SUBLANES - 1 + tm, :]
    prev2 = cu_ref[_SUBLANES - 2:_SUBLANES - 2 + tm, :]
    cw = cw_ref[...]
    y = cw[2:3, :] * cu + cw[0:1, :] * prev2 + cw[1:2, :] * prev1
    conv_ref[...] = (seg(_OFF_BGATE, CONV_DIM) * y).astype(_BF16)
    cu_ref[0:_SUBLANES, :] = cu_ref[tm:tm + _SUBLANES, :]

    q_ref[...] = (seg(_OFF_Q, SWA_DIM) * (HEAD_DIM ** -0.5)).astype(_BF16)
    k_ref[...] = seg(_OFF_K, KV_DIM).astype(_BF16)
    v_ref[...] = seg(_OFF_V, KV_DIM).astype(_BF16)
    qm_ref[...] = seg(_OFF_QM, MEM_DIM).astype(_BF16)


def _proj(h, w_in, b_in, conv_w, seq, casts=()):
    t, d = h.shape
    tm = _ROW_TM
    assert seq % tm == 0
    row = lambda i: (i, 0)
    const = lambda i: (0, 0)
    widths = (CONV_DIM, SWA_DIM, KV_DIM, KV_DIM, MEM_DIM)
    cast_specs = _cast_specs(casts, t // tm, lambda i: i)
    return pl.pallas_call(
        functools.partial(_proj_kernel, seq // tm, len(casts)),
        out_shape=[jax.ShapeDtypeStruct((t, w), _BF16) for w in widths]
        + [jax.ShapeDtypeStruct(w.shape, _BF16) for w in casts],
        grid=(t // tm,),
        in_specs=[
            pl.BlockSpec((tm, d), row),
            pl.BlockSpec((d, PROJ_DIM), const, pipeline_mode=pl.Buffered(1)),
            pl.BlockSpec((1, PROJ_DIM), const),
            pl.BlockSpec((CONV_WIDTH, CONV_DIM), const),
        ] + cast_specs,
        out_specs=[pl.BlockSpec((tm, w), row) for w in widths] + cast_specs,
        scratch_shapes=[pltpu.VMEM((tm + _SUBLANES, CONV_DIM), _F32)],
        compiler_params=pltpu.CompilerParams(
            dimension_semantics=("arbitrary",), vmem_limit_bytes=_VMEM_LIMIT),
        name="proj_conv",
    )(h, w_in, b_in, conv_w, *casts)


def _mem_kv_kernel(m_ref, w_ref, k_ref, v_ref):
    kv = _dot(m_ref[...].astype(_BF16), w_ref[...])
    k_ref[...] = kv[:, :MEM_DIM].astype(_BF16)
    v_ref[...] = kv[:, MEM_DIM:].astype(_BF16)


def _mem_kv(mem, w):
    t, d = mem.shape
    tm = MEM_LEN
    row = lambda i: (i, 0)
    return pl.pallas_call(
        _mem_kv_kernel,
        out_shape=[jax.ShapeDtypeStruct((t, MEM_DIM), _BF16)] * 2,
        grid=(t // tm,),
        in_specs=[pl.BlockSpec((tm, d), row),
                  pl.BlockSpec((d, 2 * MEM_DIM), lambda i: (0, 0))],
        out_specs=[pl.BlockSpec((tm, MEM_DIM), row)] * 2,
        compiler_params=pltpu.CompilerParams(
            dimension_semantics=("parallel",), vmem_limit_bytes=_VMEM_LIMIT),
        name="mem_kv",
    )(mem, w)


def _t5_bucket_table():
    qi = np.arange(BLOCK)[:, None]
    kj = np.arange(2 * BLOCK)[None, :]
    dist = qi + BLOCK - kj
    in_window = (dist >= 0) & (dist < WINDOW)
    dc = np.maximum(dist, 0)
    max_exact = N_BUCKETS // 2
    dl = np.maximum(dc, 1).astype(np.float32)
    large = max_exact + (np.log(dl / max_exact) / math.log(MAX_DISTANCE / max_exact)
                         * (N_BUCKETS - max_exact)).astype(np.int32)
    large = np.minimum(large, N_BUCKETS - 1)
    bucket = np.where(dc < max_exact, dc, large)
    return np.where(in_window, bucket, -1).astype(np.int32)


def _swa_kernel(q_ref, kp_ref, kc_ref, vp_ref, vc_ref, bucket_ref, relb_ref, sink_ref,
                o_ref, bias_ref):
    b = pl.program_id(0)
    n = pl.program_id(1)

    @pl.when((b == 0) & (n == 0))
    def _():
        bucket = bucket_ref[...]
        is_prev = lax.broadcasted_iota(jnp.int32, bucket.shape, 1) < BLOCK
        for h in range(SWA_HEADS):
            t = jnp.full(bucket.shape, NEG_INF, _F32)
            for j in range(N_BUCKETS):
                t = jnp.where(bucket == j, relb_ref[j, h], t)
            bias_ref[0, h] = t
            bias_ref[1, h] = jnp.where(is_prev, NEG_INF, t)

    first = (n == 0).astype(jnp.int32)
    kband = jnp.concatenate([kp_ref[...], kc_ref[...]], axis=0)
    vband = jnp.concatenate([vp_ref[...], vc_ref[...]], axis=0)

    logits = []
    for g in range(SWA_KV_HEADS):
        qg = jnp.concatenate(
            [q_ref[:, h * HEAD_DIM:(h + 1) * HEAD_DIM]
             for h in range(g * SWA_GROUP, (g + 1) * SWA_GROUP)], axis=0)
        logits.append(_dot_nt(qg, kband[:, g * HEAD_DIM:(g + 1) * HEAD_DIM]))

    probs, denoms = [], []
    for g in range(SWA_KV_HEADS):
        pg = []
        for j in range(SWA_GROUP):
            h = g * SWA_GROUP + j
            sink = sink_ref[h]
            s = logits[g][j * BLOCK:(j + 1) * BLOCK] + bias_ref[first, h]
            m = jnp.maximum(jnp.max(s, axis=-1, keepdims=True), sink)
            p = jnp.exp(s - m)
            denoms.append(jnp.sum(p, axis=-1, keepdims=True) + jnp.exp(sink - m))
            pg.append(p.astype(_BF16))
        probs.append(jnp.concatenate(pg, axis=0))

    outs = []
    for g in range(SWA_KV_HEADS):
        og = _dot(probs[g], vband[:, g * HEAD_DIM:(g + 1) * HEAD_DIM])
        for j in range(SWA_GROUP):
            outs.append(og[j * BLOCK:(j + 1) * BLOCK] / denoms[g * SWA_GROUP + j])
    o_ref[...] = jnp.concatenate(outs, axis=-1).astype(_BF16)


def _swa(q, k, v, rel_bias, sinks, batch, seq):
    nb = seq // BLOCK
    cur = lambda b, n: (b * nb + n, 0)
    prev = lambda b, n: (b * nb + jnp.maximum(n - 1, 0), 0)
    bucket = jnp.asarray(_t5_bucket_table())
    smem = pl.BlockSpec(memory_space=pltpu.SMEM)
    return pl.pallas_call(
        _swa_kernel,
        out_shape=jax.ShapeDtypeStruct((batch * seq, SWA_DIM), _BF16),
        grid=(batch, nb),
        in_specs=[
            pl.BlockSpec((BLOCK, SWA_DIM), cur),
            pl.BlockSpec((BLOCK, KV_DIM), prev),
            pl.BlockSpec((BLOCK, KV_DIM), cur),
            pl.BlockSpec((BLOCK, KV_DIM), prev),
            pl.BlockSpec((BLOCK, KV_DIM), cur),
            pl.BlockSpec((BLOCK, 2 * BLOCK), lambda b, n: (0, 0)),
            smem,
            smem,
        ],
        out_specs=pl.BlockSpec((BLOCK, SWA_DIM), cur),
        scratch_shapes=[pltpu.VMEM((2, SWA_HEADS, BLOCK, 2 * BLOCK), _F32)],
        compiler_params=pltpu.CompilerParams(
            dimension_semantics=("arbitrary", "arbitrary"), vmem_limit_bytes=_VMEM_LIMIT),
        name="swa",
    )(q, k, k, v, v, bucket, rel_bias, sinks)


def _mem_attn_kernel(q_ref, k_ref, v_ref, o_ref):
    outs = []
    for h in range(MEM_HEADS):
        sl = slice(h * MEM_HEAD_DIM, (h + 1) * MEM_HEAD_DIM)
        s = _dot_nt(q_ref[:, sl], k_ref[:, sl]) * (MEM_HEAD_DIM ** -0.5)
        p = jnp.exp(s - jnp.max(s, axis=-1, keepdims=True))
        denom = jnp.sum(p, axis=-1, keepdims=True)
        outs.append(_dot(p.astype(_BF16), v_ref[:, sl]) / denom)
    o_ref[...] = jnp.concatenate(outs, axis=-1).astype(_BF16)


def _mem_attn(qm, mem_k, mem_v, batch, seq):
    tq = _ROW_TM
    per_seq = seq // tq
    return pl.pallas_call(
        _mem_attn_kernel,
        out_shape=jax.ShapeDtypeStruct((batch * seq, MEM_DIM), _BF16),
        grid=(batch, per_seq),
        in_specs=[
            pl.BlockSpec((tq, MEM_DIM), lambda b, j: (b * per_seq + j, 0)),
            pl.BlockSpec((MEM_LEN, MEM_DIM), lambda b, j: (b, 0)),
            pl.BlockSpec((MEM_LEN, MEM_DIM), lambda b, j: (b, 0)),
        ],
        out_specs=pl.BlockSpec((tq, MEM_DIM), lambda b, j: (b * per_seq + j, 0)),
        compiler_params=pltpu.CompilerParams(
            dimension_semantics=("parallel", "parallel"), vmem_limit_bytes=_VMEM_LIMIT),
        name="mem_attn",
    )(qm, mem_k, mem_v)


def _out_ln_kernel(conv_ref, swa_ref, mem_ref, h_ref, w_ref, g_ref, b_ref, o_ref):
    for r in range(h_ref.shape[0] // _LN_ROWS):
        rows = slice(r * _LN_ROWS, (r + 1) * _LN_ROWS)
        mix = jnp.concatenate([conv_ref[rows, :], swa_ref[rows, :], mem_ref[rows, :]], axis=-1)
        y = ALPHA * h_ref[rows, :] + _dot(mix, w_ref[...])
        o_ref[rows, :] = _layer_norm(y, g_ref[...], b_ref[...])


def _out_ln(conv, swa, memo, h, w_out, g, b):
    t, d = h.shape
    tm = _OUT_TM
    row = lambda i: (i, 0)
    const = lambda i: (0, 0)
    return pl.pallas_call(
        _out_ln_kernel,
        out_shape=jax.ShapeDtypeStruct((t, d), _F32),
        grid=(t // tm,),
        in_specs=[
            pl.BlockSpec((tm, CONV_DIM), row),
            pl.BlockSpec((tm, SWA_DIM), row),
            pl.BlockSpec((tm, MEM_DIM), row),
            pl.BlockSpec((tm, d), row),
            pl.BlockSpec((MIX_DIM, d), const, pipeline_mode=pl.Buffered(1)),
            pl.BlockSpec((1, d), const),
            pl.BlockSpec((1, d), const),
        ],
        out_specs=pl.BlockSpec((tm, d), row),
        compiler_params=pltpu.CompilerParams(
            dimension_semantics=("parallel",), vmem_limit_bytes=_VMEM_LIMIT),
        name="out_ln",
    )(conv, swa, memo, h, w_out, g, b)


def kernel(x, mem, ln1_g, ln1_b, ffn1_w_gate, ffn1_w_up, ffn1_w_down, w_in, b_in, conv_w, sinks, w_mem_kv, w_out, ln2_g, ln2_b, ffn2_w_gate, ffn2_w_up, ffn2_w_down, ln3_g, ln3_b, rel_bias):
    batch, seq, d = x.shape
    h = x.reshape(batch * seq, d)
    mem2 = mem.reshape(batch * MEM_LEN, d)
    bf = lambda w: w.astype(_BF16)
    for l in range(DEPTH):
        h, w_in_b, w_mem_b, w_out_b, wd2 = _ffn_ln(
            h, bf(ffn1_w_gate[l]), bf(ffn1_w_up[l]), bf(ffn1_w_down[l]),
            ln1_g[l][None], ln1_b[l][None],
            casts=(w_in[l], w_mem_kv[l], w_out[l], ffn2_w_down[l]))
        conv, q, k, v, qm, wg2, wu2 = _proj(h, w_in_b, b_in[l][None], conv_w[l], seq,
                                            casts=(ffn2_w_gate[l], ffn2_w_up[l]))
        mem_k, mem_v = _mem_kv(mem2, w_mem_b)
        swa = _swa(q, k, v, rel_bias, sinks[l], batch, seq)
        memo = _mem_attn(qm, mem_k, mem_v, batch, seq)
        h = _out_ln(conv, swa, memo, h, w_out_b, ln2_g[l][None], ln2_b[l][None])
        h, = _ffn_ln(h, wg2, wu2, wd2, ln3_g[l][None], ln3_b[l][None])
    return h.reshape(batch, seq, d)
```
